```python
import jax, jax.numpy as jnp
from jax import lax
import numpy as np

D_MODEL = 2048
BATCH = 8
SEQ = 4096
DEPTH = 4

MIX_WIDTH = D_MODEL
GROUP_WIDTH = MIX_WIDTH // 4
HEAD_DIM = 128
N_HEADS_PER_MIXER = GROUP_WIDTH // HEAD_DIM
CHUNK = 128
SHORT_CONV = 3
CONFORMER_CONV = 31
POOL_WINDOWS = (2, 4, 8, 16)
POOL_GROUP = GROUP_WIDTH // len(POOL_WINDOWS)
D_FF = ((8 * D_MODEL // 3 + 255) // 256) * 256
PLE_DIM = 256
EPS = 1e-6
A_COLS = 2 * GROUP_WIDTH
B_COLS = 3 * GROUP_WIDTH
C_COLS = 2 * GROUP_WIDTH
D_COLS = GROUP_WIDTH
IN_COLS = A_COLS + B_COLS + C_COLS + D_COLS

kernel_name = "hybrid_sgu_conv_conformer_pool_trunk"


def _rms(x, g):
    xf = x.astype(jnp.float32)
    y = xf * lax.rsqrt(jnp.mean(xf * xf, axis=-1, keepdims=True) + EPS)
    return (y * g.astype(jnp.float32)).astype(x.dtype)


def _ln(x, g, b):
    xf = x.astype(jnp.float32)
    mu = jnp.mean(xf, axis=-1, keepdims=True)
    xc = xf - mu
    var = jnp.mean(xc * xc, axis=-1, keepdims=True)
    y = xc * lax.rsqrt(var + EPS) * g.astype(jnp.float32) + b.astype(jnp.float32)
    return y.astype(x.dtype)


def _causal_dwconv(x, w):
    k, c = w.shape
    return lax.conv_general_dilated(
        x, w[:, None, :].astype(x.dtype), window_strides=(1,), padding=[(k - 1, 0)],
        dimension_numbers=("NWC", "WIO", "NWC"), feature_group_count=c)


def _mixer_sgu(z, ln_g, ln_b, w_s, b_s):
    bsz, s, _ = z.shape
    n = s // CHUNK
    z = jax.nn.gelu(z)
    u, v = jnp.split(z, 2, axis=-1)
    v = _ln(v.reshape(bsz, s, N_HEADS_PER_MIXER, HEAD_DIM), ln_g, ln_b)
    v = v.reshape(bsz, n, CHUNK, N_HEADS_PER_MIXER, HEAD_DIM)
    mask = jnp.tril(jnp.ones((CHUNK, CHUNK), dtype=bool))
    wm = jnp.where(mask[None], w_s, jnp.zeros((), w_s.dtype))
    sp = jnp.einsum("hts,bnshd->bnthd", wm, v) + b_s.T[:, :, None]
    out = u.reshape(bsz, n, CHUNK, N_HEADS_PER_MIXER, HEAD_DIM) * sp
    return out.reshape(bsz, s, GROUP_WIDTH)


def _mixer_shortconv(z, conv_w):
    h, bg, cg = jnp.split(z, 3, axis=-1)
    return bg * _causal_dwconv(cg * h, conv_w)


def _mixer_conformer(z, conv_w, conv_b, ln_g, ln_b):
    a, g = jnp.split(z, 2, axis=-1)
    h = a * jax.nn.sigmoid(g)
    h = _causal_dwconv(h, conv_w) + conv_b
    h = _ln(h, ln_g, ln_b)
    return jax.nn.silu(h)


def _mixer_pool(z, pool_w, pool_scale):
    bsz, s, _ = z.shape
    zg = z.reshape(bsz, s, len(POOL_WINDOWS), POOL_GROUP)
    cs = jnp.cumsum(zg.astype(jnp.float32), axis=1)
    pos = jnp.arange(1, s + 1, dtype=jnp.int32)
    pooled = []
    for gi, w in enumerate(POOL_WINDOWS):
        c = cs[:, :, gi]
        lagged = jnp.pad(c, ((0, 0), (w, 0), (0, 0)))[:, :s]
        count = jnp.minimum(pos, w).astype(jnp.float32)
        pooled.append((c - lagged) / count[None, :, None])
    pooled = jnp.stack(pooled, axis=2).astype(z.dtype) - zg
    out = jnp.einsum("bsgc,gcd->bsgd", pooled, pool_w).reshape(bsz, s, GROUP_WIDTH)
    return out * pool_scale


def setup_inputs(seed: int = 0) -> dict:
    key = jax.random.key(seed)
    ks = jax.random.split(key, 24)
    f = jnp.float32
    L, D, G, H, hd = DEPTH, D_MODEL, GROUP_WIDTH, N_HEADS_PER_MIXER, HEAD_DIM
    nrm = lambda k, shape, scale: jax.random.normal(k, shape, f) * scale
    gain = lambda k, shape: 1.0 + 0.05 * jax.random.normal(k, shape, f)
    return {
        "x": jax.random.normal(ks[0], (BATCH, SEQ, D), f),
        "p": jax.random.normal(ks[1], (DEPTH, BATCH, SEQ, PLE_DIM), f),
        "norm_mix_g": gain(ks[2], (L, D)),
        "w_in": nrm(ks[3], (L, D, IN_COLS), D ** -0.5),
        "sgu_ln_g": gain(ks[4], (L, H, hd)),
        "sgu_ln_b": nrm(ks[5], (L, H, hd), 0.02),
        "sgu_w": nrm(ks[6], (L, H, CHUNK, CHUNK), CHUNK ** -0.5),
        "sgu_b": gain(ks[7], (L, H, CHUNK)),
        "sc_conv_w": nrm(ks[8], (L, SHORT_CONV, G), SHORT_CONV ** -0.5),
        "cf_conv_w": nrm(ks[9], (L, CONFORMER_CONV, G), CONFORMER_CONV ** -0.5),
        "cf_conv_b": nrm(ks[10], (L, G), 0.02),
        "cf_ln_g": gain(ks[11], (L, G)),
        "cf_ln_b": nrm(ks[12], (L, G), 0.02),
        "pool_w": nrm(ks[13], (L, len(POOL_WINDOWS), POOL_GROUP, POOL_GROUP), POOL_GROUP ** -0.5),
        "pool_scale": 0.5 + 0.1 * jax.random.normal(ks[14], (L, G), f),
        "w_out": nrm(ks[15], (L, MIX_WIDTH, D), MIX_WIDTH ** -0.5),
        "norm_ffn_g": gain(ks[16], (L, D)),
        "w_gate": nrm(ks[17], (L, D, D_FF), D ** -0.5),
        "w_up": nrm(ks[18], (L, D, D_FF), D ** -0.5),
        "w_down": nrm(ks[19], (L, D_FF, D), D_FF ** -0.5),
        "norm_ple_g": gain(ks[20], (L, D)),
        "w_ple_gate": nrm(ks[21], (L, D, D), D ** -0.5),
        "w_ple_proj": nrm(ks[22], (L, PLE_DIM, D), PLE_DIM ** -0.5),
        "final_norm_g": gain(ks[23], (D,)),
    }


def reference(x, p, norm_mix_g, w_in, sgu_ln_g, sgu_ln_b, sgu_w, sgu_b, sc_conv_w,
              cf_conv_w, cf_conv_b, cf_ln_g, cf_ln_b, pool_w, pool_scale, w_out,
              norm_ffn_g, w_gate, w_up, w_down, norm_ple_g, w_ple_gate, w_ple_proj,
              final_norm_g):
    h = x
    split_at = [A_COLS, A_COLS + B_COLS, A_COLS + B_COLS + C_COLS]
    for i in range(DEPTH):
        y = _rms(h, norm_mix_g[i])
        z = y @ w_in[i]
        za, zb, zc, zd = jnp.split(z, split_at, axis=-1)
        oa = _mixer_sgu(za, sgu_ln_g[i], sgu_ln_b[i], sgu_w[i], sgu_b[i])
        ob = _mixer_shortconv(zb, sc_conv_w[i])
        oc = _mixer_conformer(zc, cf_conv_w[i], cf_conv_b[i], cf_ln_g[i], cf_ln_b[i])
        od = _mixer_pool(zd, pool_w[i], pool_scale[i])
        h = h + jnp.concatenate([oa, ob, oc, od], axis=-1) @ w_out[i]
        y = _rms(h, norm_ffn_g[i])
        h = h + (jax.nn.silu(y @ w_gate[i]) * (y @ w_up[i])) @ w_down[i]
        y = _rms(h, norm_ple_g[i])
        h = h + jax.nn.sigmoid(y @ w_ple_gate[i]) * (p[i] @ w_ple_proj[i])
    return _rms(h, final_norm_g)
```

```python
import functools

import jax
import jax.numpy as jnp
from jax import lax
from jax.experimental import pallas as pl
from jax.experimental.pallas import tpu as pltpu

D_MODEL = 2048
GROUP_WIDTH = D_MODEL // 4
HEAD_DIM = 128
N_HEADS = GROUP_WIDTH // HEAD_DIM
CHUNK = 128
SHORT_CONV = 3
CONFORMER_CONV = 31
POOL_WINDOWS = (2, 4, 8, 16)
POOL_GROUP = GROUP_WIDTH // len(POOL_WINDOWS)
D_FF = 5632
PLE_DIM = 256
EPS = 1e-6
A_COLS = 2 * GROUP_WIDTH
B_COLS = 3 * GROUP_WIDTH
C_COLS = 2 * GROUP_WIDTH
D_COLS = GROUP_WIDTH
IN_COLS = A_COLS + B_COLS + C_COLS + D_COLS
A_OFF, B_OFF, C_OFF, D_OFF = 0, A_COLS, A_COLS + B_COLS, A_COLS + B_COLS + C_COLS

SUBLANE = 8
LANE = 128
V7X_VMEM_BYTES = 64 * 1024 * 1024
MIB = 1024 * 1024


def _round_up(x, m):
    return (x + m - 1) // m * m


HALO_B = _round_up(SHORT_CONV - 1, SUBLANE)
HALO_C = _round_up(CONFORMER_CONV - 1, SUBLANE)
HALO_D = _round_up(max(POOL_WINDOWS) - 1, SUBLANE)

MIX_TM = 512
PROJ_TM = 512
FFN_TM = 512
FFN_TF = 512
PLE_TM = 512
CONV_ROWS = 64


def _vmem_limit(nbytes):
    return int(min(nbytes, V7X_VMEM_BYTES - 4 * MIB))


def _rms(h, g):
    ms = jnp.mean(h * h, axis=-1, keepdims=True)
    return h * lax.rsqrt(ms + EPS) * g


def _dot(a, b):
    return jnp.dot(a, b, preferred_element_type=jnp.float32)


def _mix_kernel(h_ref, g_ref, w_in_ref, sgu_g_ref, sgu_b_ref, sgu_w_ref, sgu_bias_ref, sc_w_ref,
                cf_w_ref, cf_b_ref, cf_g_ref, cf_beta_ref, pool_w_ref, pool_s_ref,
                o_ref, buf_b, buf_c, buf_d, *, tiles_per_seq):
    tm = h_ref.shape[0]
    tile_in_seq = pl.program_id(0) % tiles_per_seq

    @pl.when(tile_in_seq == 0)
    def _():
        buf_b[0:HALO_B, :] = jnp.zeros((HALO_B, GROUP_WIDTH), jnp.float32)
        buf_c[0:HALO_C, :] = jnp.zeros((HALO_C, GROUP_WIDTH), jnp.float32)
        buf_d[0:HALO_D, :] = jnp.zeros((HALO_D, GROUP_WIDTH), jnp.float32)

    y = _rms(h_ref[...], g_ref[...]).astype(jnp.bfloat16)

    za = jax.nn.gelu(_dot(y, w_in_ref[:, A_OFF:A_OFF + A_COLS]))
    row = lax.broadcasted_iota(jnp.int32, (CHUNK, CHUNK), 0)
    col = lax.broadcasted_iota(jnp.int32, (CHUNK, CHUNK), 1)
    causal = row >= col
    for hd in range(N_HEADS):
        c0 = hd * HEAD_DIM
        u = za[:, c0:c0 + HEAD_DIM]
        v = za[:, GROUP_WIDTH + c0:GROUP_WIDTH + c0 + HEAD_DIM]
        mu = jnp.mean(v, axis=-1, keepdims=True)
        vc = v - mu
        var = jnp.mean(vc * vc, axis=-1, keepdims=True)
        vn = vc * lax.rsqrt(var + EPS) * sgu_g_ref[hd:hd + 1, :] + sgu_b_ref[hd:hd + 1, :]
        vn = vn.astype(jnp.bfloat16)
        wm = jnp.where(causal, sgu_w_ref[hd], 0.0).astype(jnp.bfloat16)
        bias = sgu_bias_ref[:, hd:hd + 1]
        for c in range(tm // CHUNK):
            r0 = c * CHUNK
            sp = _dot(wm, vn[r0:r0 + CHUNK, :]) + bias
            o_ref[r0:r0 + CHUNK, c0:c0 + HEAD_DIM] = (u[r0:r0 + CHUNK, :] * sp).astype(o_ref.dtype)

    zb = _dot(y, w_in_ref[:, B_OFF:B_OFF + B_COLS])
    buf_b[HALO_B:HALO_B + tm, :] = zb[:, 2 * GROUP_WIDTH:3 * GROUP_WIDTH] * zb[:, 0:GROUP_WIDTH]
    for r0 in range(0, tm, CONV_ROWS):
        acc = None
        for k in range(SHORT_CONV):
            start = HALO_B + r0 - (SHORT_CONV - 1) + k
            term = sc_w_ref[k:k + 1, :] * buf_b[start:start + CONV_ROWS, :]
            acc = term if acc is None else acc + term
        gate = zb[r0:r0 + CONV_ROWS, GROUP_WIDTH:2 * GROUP_WIDTH]
        o_ref[r0:r0 + CONV_ROWS, GROUP_WIDTH:2 * GROUP_WIDTH] = (gate * acc).astype(o_ref.dtype)
    buf_b[0:HALO_B, :] = buf_b[tm:tm + HALO_B, :]

    zc = _dot(y, w_in_ref[:, C_OFF:C_OFF + C_COLS])
    buf_c[HALO_C:HALO_C + tm, :] = zc[:, 0:GROUP_WIDTH] * jax.nn.sigmoid(zc[:, GROUP_WIDTH:2 * GROUP_WIDTH])
    for r0 in range(0, tm, CONV_ROWS):
        parts = []
        for c0 in range(0, GROUP_WIDTH, LANE):
            acc = None
            for k in range(CONFORMER_CONV):
                start = HALO_C + r0 - (CONFORMER_CONV - 1) + k
                term = cf_w_ref[k:k + 1, c0:c0 + LANE] * buf_c[start:start + CONV_ROWS, c0:c0 + LANE]
                acc = term if acc is None else acc + term
            parts.append(acc + cf_b_ref[:, c0:c0 + LANE])
        hc = jnp.concatenate(parts, axis=-1)
        mu = jnp.mean(hc, axis=-1, keepdims=True)
        xc = hc - mu
        var = jnp.mean(xc * xc, axis=-1, keepdims=True)
        hn = xc * lax.rsqrt(var + EPS) * cf_g_ref[...] + cf_beta_ref[...]
        o_ref[r0:r0 + CONV_ROWS, 2 * GROUP_WIDTH:3 * GROUP_WIDTH] = jax.nn.silu(hn).astype(o_ref.dtype)
    buf_c[0:HALO_C, :] = buf_c[tm:tm + HALO_C, :]

    zd = _dot(y, w_in_ref[:, D_OFF:D_OFF + D_COLS])
    buf_d[HALO_D:HALO_D + tm, :] = zd
    pos = tile_in_seq * tm + lax.broadcasted_iota(jnp.int32, (tm, 1), 0) + 1
    for gi, win in enumerate(POOL_WINDOWS):
        c0 = gi * POOL_GROUP
        zg = zd[:, c0:c0 + POOL_GROUP]
        acc = zg
        for k in range(1, win):
            acc = acc + buf_d[HALO_D - k:HALO_D - k + tm, c0:c0 + POOL_GROUP]
        count = jnp.minimum(pos, win).astype(jnp.float32)
        pooled = (acc / count - zg).astype(jnp.bfloat16)
        od = _dot(pooled, pool_w_ref[gi]) * pool_s_ref[:, c0:c0 + POOL_GROUP]
        o_ref[:, 3 * GROUP_WIDTH + c0:3 * GROUP_WIDTH + c0 + POOL_GROUP] = od.astype(o_ref.dtype)
    buf_d[0:HALO_D, :] = buf_d[tm:tm + HALO_D, :]


def _mix_call(h, layer, seq_len, g, w_in, sgu_g, sgu_b, sgu_w, sgu_bias, sc_w, cf_w, cf_b, cf_g, cf_beta,
              pool_w, pool_s):
    m = h.shape[0]
    tm = MIX_TM
    assert m % tm == 0 and seq_len % tm == 0 and tm % CHUNK == 0 and tm % CONV_ROWS == 0
    const = lambda *shape: pl.BlockSpec((None,) + shape, lambda i: (layer,) + (0,) * len(shape),
                                        pipeline_mode=pl.Buffered(1))
    vmem = (2 * tm * D_MODEL * 4 + 2 * tm * D_MODEL * 2 + D_MODEL * IN_COLS * 2
            + 3 * (tm + HALO_C) * GROUP_WIDTH * 4 + 6 * tm * B_COLS * 4 + 4 * MIB)
    return pl.pallas_call(
        functools.partial(_mix_kernel, tiles_per_seq=seq_len // tm),
        grid=(m // tm,),
        in_specs=[
            pl.BlockSpec((tm, D_MODEL), lambda i: (i, 0)),
            const(1, D_MODEL),
            const(D_MODEL, IN_COLS),
            const(N_HEADS, HEAD_DIM),
            const(N_HEADS, HEAD_DIM),
            const(N_HEADS, CHUNK, CHUNK),
            const(CHUNK, N_HEADS),
            const(SHORT_CONV, GROUP_WIDTH),
            const(CONFORMER_CONV, GROUP_WIDTH),
            const(1, GROUP_WIDTH),
            const(1, GROUP_WIDTH),
            const(1, GROUP_WIDTH),
            const(len(POOL_WINDOWS), POOL_GROUP, POOL_GROUP),
            const(1, GROUP_WIDTH),
        ],
        out_specs=pl.BlockSpec((tm, D_MODEL), lambda i: (i, 0)),
        out_shape=jax.ShapeDtypeStruct((m, D_MODEL), jnp.bfloat16),
        scratch_shapes=[
            pltpu.VMEM((HALO_B + tm, GROUP_WIDTH), jnp.float32),
            pltpu.VMEM((HALO_C + tm, GROUP_WIDTH), jnp.float32),
            pltpu.VMEM((HALO_D + tm, GROUP_WIDTH), jnp.float32),
        ],
        compiler_params=pltpu.CompilerParams(
            dimension_semantics=("arbitrary",), vmem_limit_bytes=_vmem_limit(vmem)),
        name="mix",
    )(h, g, w_in, sgu_g, sgu_b, sgu_w, sgu_bias, sc_w, cf_w, cf_b, cf_g, cf_beta, pool_w, pool_s)


def _proj_kernel(c_ref, h_ref, w_ref, o_ref):
    o_ref[...] = h_ref[...] + _dot(c_ref[...], w_ref[...])


def _proj_call(c, h, layer, w_out):
    m = h.shape[0]
    tm = PROJ_TM
    assert m % tm == 0
    vmem = 2 * tm * D_MODEL * 2 + 4 * tm * D_MODEL * 4 + D_MODEL * D_MODEL * 2 + tm * D_MODEL * 4 + 4 * MIB
    return pl.pallas_call(
        _proj_kernel,
        grid=(m // tm,),
        in_specs=[
            pl.BlockSpec((tm, D_MODEL), lambda i: (i, 0)),
            pl.BlockSpec((tm, D_MODEL), lambda i: (i, 0)),
            pl.BlockSpec((None, D_MODEL, D_MODEL), lambda i: (layer, 0, 0), pipeline_mode=pl.Buffered(1)),
        ],
        out_specs=pl.BlockSpec((tm, D_MODEL), lambda i: (i, 0)),
        out_shape=jax.ShapeDtypeStruct((m, D_MODEL), jnp.float32),
        compiler_params=pltpu.CompilerParams(
            dimension_semantics=("parallel",), vmem_limit_bytes=_vmem_limit(vmem)),
        name="proj",
    )(c, h, w_out)


def _ffn_kernel(h_ref, g_ref, wg_ref, wu_ref, wd_ref, o_ref, y_scr):
    @pl.when(pl.program_id(1) == 0)
    def _():
        h = h_ref[...]
        y_scr[...] = _rms(h, g_ref[...]).astype(y_scr.dtype)
        o_ref[...] = h

    y = y_scr[...]
    hid = (jax.nn.silu(_dot(y, wg_ref[...])) * _dot(y, wu_ref[...])).astype(jnp.bfloat16)
    o_ref[...] += _dot(hid, wd_ref[...])


def _ffn_call(h, layer, g, w_gate, w_up, w_down):
    m = h.shape[0]
    tm, tf = FFN_TM, FFN_TF
    assert m % tm == 0 and D_FF % tf == 0
    vmem = (4 * tm * D_MODEL * 4 + tm * D_MODEL * 2 + 3 * 2 * D_MODEL * tf * 2
            + 3 * tm * tf * 4 + tm * D_MODEL * 4 + 4 * MIB)
    return pl.pallas_call(
        _ffn_kernel,
        grid=(m // tm, D_FF // tf),
        in_specs=[
            pl.BlockSpec((tm, D_MODEL), lambda i, j: (i, 0)),
            pl.BlockSpec((None, 1, D_MODEL), lambda i, j: (layer, 0, 0), pipeline_mode=pl.Buffered(1)),
            pl.BlockSpec((None, D_MODEL, tf), lambda i, j: (layer, 0, j)),
            pl.BlockSpec((None, D_MODEL, tf), lambda i, j: (layer, 0, j)),
            pl.BlockSpec((None, tf, D_MODEL), lambda i, j: (layer, j, 0)),
        ],
        out_specs=pl.BlockSpec((tm, D_MODEL), lambda i, j: (i, 0)),
        out_shape=jax.ShapeDtypeStruct((m, D_MODEL), jnp.float32),
        scratch_shapes=[pltpu.VMEM((tm, D_MODEL), jnp.bfloat16)],
        compiler_params=pltpu.CompilerParams(
            dimension_semantics=("parallel", "arbitrary"), vmem_limit_bytes=_vmem_limit(vmem)),
        name="ffn",
    )(h, g, w_gate, w_up, w_down)


def _ple_kernel(h_ref, p_ref, g_ref, wg_ref, wp_ref, fg_ref, o_ref, *, final_norm):
    h = h_ref[...]
    y = _rms(h, g_ref[...]).astype(jnp.bfloat16)
    gate = jax.nn.sigmoid(_dot(y, wg_ref[...]))
    out = h + gate * _dot(p_ref[...].astype(jnp.bfloat16), wp_ref[...])
    if final_norm:
        out = _rms(out, fg_ref[...])
    o_ref[...] = out


def _ple_call(h, p, layer, g, w_gate, w_proj, final_g, final_norm):
    m = h.shape[0]
    tm = PLE_TM
    assert m % tm == 0
    vmem = (4 * tm * D_MODEL * 4 + 2 * tm * PLE_DIM * 4 + D_MODEL * D_MODEL * 2 + PLE_DIM * D_MODEL * 2
            + 3 * tm * D_MODEL * 4 + 4 * MIB)
    return pl.pallas_call(
        functools.partial(_ple_kernel, final_norm=final_norm),
        grid=(m // tm,),
        in_specs=[
            pl.BlockSpec((tm, D_MODEL), lambda i: (i, 0)),
            pl.BlockSpec((None, tm, PLE_DIM), lambda i: (layer, i, 0)),
            pl.BlockSpec((None, 1, D_MODEL), lambda i: (layer, 0, 0), pipeline_mode=pl.Buffered(1)),
            pl.BlockSpec((None, D_MODEL, D_MODEL), lambda i: (layer, 0, 0), pipeline_mode=pl.Buffered(1)),
            pl.BlockSpec((None, PLE_DIM, D_MODEL), lambda i: (layer, 0, 0), pipeline_mode=pl.Buffered(1)),
            pl.BlockSpec((1, D_MODEL), lambda i: (0, 0), pipeline_mode=pl.Buffered(1)),
        ],
        out_specs=pl.BlockSpec((tm, D_MODEL), lambda i: (i, 0)),
        out_shape=jax.ShapeDtypeStruct((m, D_MODEL), jnp.float32),
        compiler_params=pltpu.CompilerParams(
            dimension_semantics=("parallel",), vmem_limit_bytes=_vmem_limit(vmem)),
        name="ple",
    )(h, p, g, w_gate, w_proj, final_g)


def kernel(x, p, norm_mix_g, w_in, sgu_ln_g, sgu_ln_b, sgu_w, sgu_b, sc_conv_w, cf_conv_w, cf_conv_b,
           cf_ln_g, cf_ln_b, pool_w, pool_scale, w_out, norm_ffn_g, w_gate, w_up, w_down, norm_ple_g,
           w_ple_gate, w_ple_proj, final_norm_g):
    batch, seq_len, d_model = x.shape
    depth = w_in.shape[0]
    m = batch * seq_len
    bf16 = jnp.bfloat16
    row = lambda a: a.reshape(depth, 1, a.shape[-1])

    h = x.reshape(m, d_model)
    p2 = p.reshape(depth, m, PLE_DIM)
    w_in_b, w_out_b = w_in.astype(bf16), w_out.astype(bf16)
    w_gate_b, w_up_b, w_down_b = w_gate.astype(bf16), w_up.astype(bf16), w_down.astype(bf16)
    w_pg_b, w_pp_b, pool_w_b = w_ple_gate.astype(bf16), w_ple_proj.astype(bf16), pool_w.astype(bf16)
    sgu_bias = jnp.swapaxes(sgu_b, 1, 2)
    mix_g, ffn_g, ple_g = row(norm_mix_g), row(norm_ffn_g), row(norm_ple_g)
    cf_b, cf_g, cf_beta, pool_s = row(cf_conv_b), row(cf_ln_g), row(cf_ln_b), row(pool_scale)
    final_g = final_norm_g.reshape(1, d_model)

    for layer in range(depth):
        c = _mix_call(h, layer, seq_len, mix_g, w_in_b, sgu_ln_g, sgu_ln_b, sgu_w, sgu_bias, sc_conv_w,
                      cf_conv_w, cf_b, cf_g, cf_beta, pool_w_b, pool_s)
        h = _proj_call(c, h, layer, w_out_b)
        h = _ffn_call(h, layer, ffn_g, w_gate_b, w_up_b, w_down_b)
        h = _ple_call(h, p2, layer, ple_g, w_pg_b, w_pp_b, final_g, final_norm=layer == depth - 1)
    return h.reshape(batch, seq_len, d_model)
```

```python
import functools

import jax
import jax.numpy as jnp
from jax import lax
from jax.experimental import pallas as pl
from jax.experimental.pallas import tpu as pltpu

D_MODEL = 2048
GROUP_WIDTH = D_MODEL // 4
HEAD_DIM = 128
N_HEADS = GROUP_WIDTH // HEAD_DIM
CHUNK = 128
SHORT_CONV = 3
CONFORMER_CONV = 31
POOL_WINDOWS = (2, 4, 8, 16)
POOL_GROUP = GROUP_WIDTH // len(POOL_WINDOWS)
D_FF = 5632
PLE_DIM = 256
EPS = 1e-6
A_COLS = 2 * GROUP_WIDTH
B_COLS = 3 * GROUP_WIDTH
C_COLS = 2 * GROUP_WIDTH
D_COLS = GROUP_WIDTH
IN_COLS = A_COLS + B_COLS + C_COLS + D_COLS
A_OFF, B_OFF, C_OFF, D_OFF = 0, A_COLS, A_COLS + B_COLS, A_COLS + B_COLS + C_COLS

SUBLANE = 8
LANE = 128
V7X_VMEM_BYTES = 64 * 1024 * 1024
MIB = 1024 * 1024


def _round_up(x, m):
    return (x + m - 1) // m * m


HALO_B = _round_up(SHORT_CONV - 1, SUBLANE)
HALO_C = _round_up(CONFORMER_CONV - 1, SUBLANE)
HALO_D = _round_up(max(POOL_WINDOWS) - 1, SUBLANE)

MIX_TM = 512
PROJ_TM = 512
FFN_TM = 1024
FFN_TF = 512
PLE_TM = 512
CONV_ROWS = 64


def _vmem_limit(nbytes):
    return int(min(nbytes, V7X_VMEM_BYTES - 4 * MIB))


def _rms(h, g):
    ms = jnp.mean(h * h, axis=-1, keepdims=True)
    return h * lax.rsqrt(ms + EPS) * g


def _layer_norm(x, g, b):
    mu = jnp.mean(x, axis=-1, keepdims=True)
    xc = x - mu
    var = jnp.mean(xc * xc, axis=-1, keepdims=True)
    return xc * lax.rsqrt(var + EPS) * g + b


def _dot(a, b):
    return jnp.dot(a, b, preferred_element_type=jnp.float32)


def _delay_rows(x, s):
    return x if s == 0 else pltpu.roll(x, s, 0)


def _depthwise_causal(x_ext, w_ref, n_taps, halo, rows, c0):
    acc = None
    for q in range(min(SUBLANE, n_taps)):
        xq = _delay_rows(x_ext, q)
        for s in range(q, n_taps, SUBLANE):
            k = n_taps - 1 - s
            lo = halo - (s - q)
            term = w_ref[k:k + 1, c0:c0 + LANE] * xq[lo:lo + rows, :]
            acc = term if acc is None else acc + term
    return acc


def _mix_kernel(h_ref, g_ref, w_in_ref, sgu_g_ref, sgu_b_ref, sgu_w_ref, sgu_bias_ref, sc_w_ref,
                cf_w_ref, cf_b_ref, cf_g_ref, cf_beta_ref, pool_w_ref, pool_s_ref,
                o_ref, buf_b, buf_c, buf_d, pooled_scr, *, tiles_per_seq):
    tm = h_ref.shape[0]
    rows = CONV_ROWS
    tile_in_seq = pl.program_id(0) % tiles_per_seq

    @pl.when(tile_in_seq == 0)
    def _():
        buf_b[0:HALO_B, :] = jnp.zeros((HALO_B, GROUP_WIDTH), jnp.float32)
        buf_c[0:HALO_C, :] = jnp.zeros((HALO_C, GROUP_WIDTH), jnp.float32)
        buf_d[0:HALO_D, :] = jnp.zeros((HALO_D, GROUP_WIDTH), jnp.float32)

    y = _rms(h_ref[...], g_ref[...]).astype(jnp.bfloat16)

    zc = _dot(y, w_in_ref[:, C_OFF:C_OFF + C_COLS])
    buf_c[HALO_C:HALO_C + tm, :] = zc[:, 0:GROUP_WIDTH] * jax.nn.sigmoid(zc[:, GROUP_WIDTH:2 * GROUP_WIDTH])
    za = jax.nn.gelu(_dot(y, w_in_ref[:, A_OFF:A_OFF + A_COLS]))
    for r0 in range(0, tm, rows):
        parts = []
        for c0 in range(0, GROUP_WIDTH, LANE):
            x_ext = buf_c[r0:r0 + HALO_C + rows, c0:c0 + LANE]
            acc = _depthwise_causal(x_ext, cf_w_ref, CONFORMER_CONV, HALO_C, rows, c0)
            parts.append(acc + cf_b_ref[:, c0:c0 + LANE])
        hn = _layer_norm(jnp.concatenate(parts, axis=-1), cf_g_ref[...], cf_beta_ref[...])
        o_ref[r0:r0 + rows, 2 * GROUP_WIDTH:3 * GROUP_WIDTH] = jax.nn.silu(hn).astype(o_ref.dtype)
    buf_c[0:HALO_C, :] = buf_c[tm:tm + HALO_C, :]

    row = lax.broadcasted_iota(jnp.int32, (CHUNK, CHUNK), 0)
    col = lax.broadcasted_iota(jnp.int32, (CHUNK, CHUNK), 1)
    causal = row >= col
    n_chunks = tm // CHUNK
    for hd in range(N_HEADS):
        c0 = hd * HEAD_DIM
        u = za[:, c0:c0 + HEAD_DIM]
        v = za[:, GROUP_WIDTH + c0:GROUP_WIDTH + c0 + HEAD_DIM]
        vn = _layer_norm(v, sgu_g_ref[hd:hd + 1, :], sgu_b_ref[hd:hd + 1, :]).astype(jnp.bfloat16)
        wm = jnp.where(causal, sgu_w_ref[hd], 0.0).astype(jnp.bfloat16)
        rhs = jnp.concatenate([vn[c * CHUNK:(c + 1) * CHUNK, :] for c in range(n_chunks)], axis=1)
        sp = _dot(wm, rhs)
        for c in range(n_chunks):
            gate = sp[:, c * HEAD_DIM:(c + 1) * HEAD_DIM] + sgu_bias_ref[hd]
            o_ref[c * CHUNK:(c + 1) * CHUNK, c0:c0 + HEAD_DIM] = (
                u[c * CHUNK:(c + 1) * CHUNK, :] * gate).astype(o_ref.dtype)

    zb = _dot(y, w_in_ref[:, B_OFF:B_OFF + B_COLS])
    buf_b[HALO_B:HALO_B + tm, :] = zb[:, 2 * GROUP_WIDTH:3 * GROUP_WIDTH] * zb[:, 0:GROUP_WIDTH]
    for r0 in range(0, tm, rows):
        for c0 in range(0, GROUP_WIDTH, LANE):
            x_ext = buf_b[r0:r0 + HALO_B + rows, c0:c0 + LANE]
            acc = _depthwise_causal(x_ext, sc_w_ref, SHORT_CONV, HALO_B, rows, c0)
            gate = zb[r0:r0 + rows, GROUP_WIDTH + c0:GROUP_WIDTH + c0 + LANE]
            o_ref[r0:r0 + rows, GROUP_WIDTH + c0:GROUP_WIDTH + c0 + LANE] = (gate * acc).astype(o_ref.dtype)
    buf_b[0:HALO_B, :] = buf_b[tm:tm + HALO_B, :]

    buf_d[HALO_D:HALO_D + tm, :] = _dot(y, w_in_ref[:, D_OFF:D_OFF + D_COLS])
    for r0 in range(0, tm, rows):
        pos = tile_in_seq * tm + r0 + 1 + lax.broadcasted_iota(jnp.int32, (rows, LANE), 0)
        for gi, win in enumerate(POOL_WINDOWS):
            c0 = gi * POOL_GROUP
            x_ext = buf_d[r0:r0 + HALO_D + rows, c0:c0 + POOL_GROUP]
            s, width = x_ext, 1
            while width < win:
                s = s + _delay_rows(s, width)
                width *= 2
            count = jnp.minimum(pos, win).astype(jnp.float32)
            pooled = s[HALO_D:HALO_D + rows, :] / count - x_ext[HALO_D:HALO_D + rows, :]
            pooled_scr[r0:r0 + rows, c0:c0 + POOL_GROUP] = pooled.astype(pooled_scr.dtype)
    buf_d[0:HALO_D, :] = buf_d[tm:tm + HALO_D, :]
    for gi in range(len(POOL_WINDOWS)):
        c0 = gi * POOL_GROUP
        od = _dot(pooled_scr[:, c0:c0 + POOL_GROUP], pool_w_ref[gi]) * pool_s_ref[:, c0:c0 + POOL_GROUP]
        o_ref[:, 3 * GROUP_WIDTH + c0:3 * GROUP_WIDTH + c0 + POOL_GROUP] = od.astype(o_ref.dtype)


def _mix_call(h, layer, seq_len, g, w_in, sgu_g, sgu_b, sgu_w, sgu_bias, sc_w, cf_w, cf_b, cf_g, cf_beta,
              pool_w, pool_s):
    m = h.shape[0]
    tm = MIX_TM
    assert m % tm == 0 and seq_len % tm == 0 and tm % CHUNK == 0 and tm % CONV_ROWS == 0
    const = lambda *shape: pl.BlockSpec((None,) + shape, lambda i: (layer,) + (0,) * len(shape),
                                        pipeline_mode=pl.Buffered(1))
    vmem = (2 * tm * D_MODEL * 4 + 2 * tm * D_MODEL * 2 + D_MODEL * IN_COLS * 2
            + 3 * (tm + HALO_C) * GROUP_WIDTH * 4 + 6 * tm * B_COLS * 4 + 4 * MIB)
    return pl.pallas_call(
        functools.partial(_mix_kernel, tiles_per_seq=seq_len // tm),
        grid=(m // tm,),
        in_specs=[
            pl.BlockSpec((tm, D_MODEL), lambda i: (i, 0)),
            const(1, D_MODEL),
            const(D_MODEL, IN_COLS),
            const(N_HEADS, HEAD_DIM),
            const(N_HEADS, HEAD_DIM),
            const(N_HEADS, CHUNK, CHUNK),
            const(N_HEADS, CHUNK, HEAD_DIM),
            const(SHORT_CONV, GROUP_WIDTH),
            const(CONFORMER_CONV, GROUP_WIDTH),
            const(1, GROUP_WIDTH),
            const(1, GROUP_WIDTH),
            const(1, GROUP_WIDTH),
            const(len(POOL_WINDOWS), POOL_GROUP, POOL_GROUP),
            const(1, GROUP_WIDTH),
        ],
        out_specs=pl.BlockSpec((tm, D_MODEL), lambda i: (i, 0)),
        out_shape=jax.ShapeDtypeStruct((m, D_MODEL), jnp.bfloat16),
        scratch_shapes=[
            pltpu.VMEM((HALO_B + tm, GROUP_WIDTH), jnp.float32),
            pltpu.VMEM((HALO_C + tm, GROUP_WIDTH), jnp.float32),
            pltpu.VMEM((HALO_D + tm, GROUP_WIDTH), jnp.float32),
            pltpu.VMEM((tm, GROUP_WIDTH), jnp.bfloat16),
        ],
        compiler_params=pltpu.CompilerParams(
            dimension_semantics=("arbitrary",), vmem_limit_bytes=_vmem_limit(vmem)),
        name="mix",
    )(h, g, w_in, sgu_g, sgu_b, sgu_w, sgu_bias, sc_w, cf_w, cf_b, cf_g, cf_beta, pool_w, pool_s)


def _proj_kernel(c_ref, h_ref, w_ref, o_ref):
    o_ref[...] = h_ref[...] + _dot(c_ref[...], w_ref[...])


def _proj_call(c, h, layer, w_out):
    m = h.shape[0]
    tm = PROJ_TM
    assert m % tm == 0
    vmem = 2 * tm * D_MODEL * 2 + 4 * tm * D_MODEL * 4 + D_MODEL * D_MODEL * 2 + tm * D_MODEL * 4 + 4 * MIB
    return pl.pallas_call(
        _proj_kernel,
        grid=(m // tm,),
        in_specs=[
            pl.BlockSpec((tm, D_MODEL), lambda i: (i, 0)),
            pl.BlockSpec((tm, D_MODEL), lambda i: (i, 0)),
            pl.BlockSpec((None, D_MODEL, D_MODEL), lambda i: (layer, 0, 0), pipeline_mode=pl.Buffered(1)),
        ],
        out_specs=pl.BlockSpec((tm, D_MODEL), lambda i: (i, 0)),
        out_shape=jax.ShapeDtypeStruct((m, D_MODEL), jnp.float32),
        compiler_params=pltpu.CompilerParams(
            dimension_semantics=("parallel",), vmem_limit_bytes=_vmem_limit(vmem)),
        name="proj",
    )(c, h, w_out)


def _ffn_kernel(h_ref, g_ref, wg_ref, wu_ref, wd_ref, o_ref, y_scr):
    @pl.when(pl.program_id(1) == 0)
    def _():
        h = h_ref[...]
        y_scr[...] = _rms(h, g_ref[...]).astype(y_scr.dtype)
        o_ref[...] = h

    y = y_scr[...]
    hid = (jax.nn.silu(_dot(y, wg_ref[...])) * _dot(y, wu_ref[...])).astype(jnp.bfloat16)
    o_ref[...] += _dot(hid, wd_ref[...])


def _ffn_call(h, layer, g, w_gate, w_up, w_down):
    m = h.shape[0]
    tm, tf = FFN_TM, FFN_TF
    assert m % tm == 0 and D_FF % tf == 0
    vmem = (4 * tm * D_MODEL * 4 + tm * D_MODEL * 2 + 3 * 2 * D_MODEL * tf * 2
            + 3 * tm * tf * 4 + tm * D_MODEL * 4 + 4 * MIB)
    return pl.pallas_call(
        _ffn_kernel,
        grid=(m // tm, D_FF // tf),
        in_specs=[
            pl.BlockSpec((tm, D_MODEL), lambda i, j: (i, 0)),
            pl.BlockSpec((None, 1, D_MODEL), lambda i, j: (layer, 0, 0), pipeline_mode=pl.Buffered(1)),
            pl.BlockSpec((None, D_MODEL, tf), lambda i, j: (layer, 0, j)),
            pl.BlockSpec((None, D_MODEL, tf), lambda i, j: (layer, 0, j)),
            pl.BlockSpec((None, tf, D_MODEL), lambda i, j: (layer, j, 0)),
        ],
        out_specs=pl.BlockSpec((tm, D_MODEL), lambda i, j: (i, 0)),
        out_shape=jax.ShapeDtypeStruct((m, D_MODEL), jnp.float32),
        scratch_shapes=[pltpu.VMEM((tm, D_MODEL), jnp.bfloat16)],
        compiler_params=pltpu.CompilerParams(
            dimension_semantics=("parallel", "arbitrary"), vmem_limit_bytes=_vmem_limit(vmem)),
        name="ffn",
    )(h, g, w_gate, w_up, w_down)


def _ple_kernel(h_ref, p_ref, g_ref, wg_ref, wp_ref, fg_ref, o_ref, *, final_norm):
    h = h_ref[...]
    y = _rms(h, g_ref[...]).astype(jnp.bfloat16)
    gate = jax.nn.sigmoid(_dot(y, wg_ref[...]))
    out = h + gate * _dot(p_ref[...].astype(jnp.bfloat16), wp_ref[...])
    if final_norm:
        out = _rms(out, fg_ref[...])
    o_ref[...] = out


def _ple_call(h, p, layer, g, w_gate, w_proj, final_g, final_norm):
    m = h.shape[0]
    tm = PLE_TM
    assert m % tm == 0
    vmem = (4 * tm * D_MODEL * 4 + 2 * tm * PLE_DIM * 4 + D_MODEL * D_MODEL * 2 + PLE_DIM * D_MODEL * 2
            + 3 * tm * D_MODEL * 4 + 4 * MIB)
    return pl.pallas_call(
        functools.partial(_ple_kernel, final_norm=final_norm),
        grid=(m // tm,),
        in_specs=[
            pl.BlockSpec((tm, D_MODEL), lambda i: (i, 0)),
            pl.BlockSpec((None, tm, PLE_DIM), lambda i: (layer, i, 0)),
            pl.BlockSpec((None, 1, D_MODEL), lambda i: (layer, 0, 0), pipeline_mode=pl.Buffered(1)),
            pl.BlockSpec((None, D_MODEL, D_MODEL), lambda i: (layer, 0, 0), pipeline_mode=pl.Buffered(1)),
            pl.BlockSpec((None, PLE_DIM, D_MODEL), lambda i: (layer, 0, 0), pipeline_mode=pl.Buffered(1)),
            pl.BlockSpec((1, D_MODEL), lambda i: (0, 0), pipeline_mode=pl.Buffered(1)),
        ],
        out_specs=pl.BlockSpec((tm, D_MODEL), lambda i: (i, 0)),
        out_shape=jax.ShapeDtypeStruct((m, D_MODEL), jnp.float32),
        compiler_params=pltpu.CompilerParams(
            dimension_semantics=("parallel",), vmem_limit_bytes=_vmem_limit(vmem)),
        name="ple",
    )(h, p, g, w_gate, w_proj, final_g)


def kernel(x, p, norm_mix_g, w_in, sgu_ln_g, sgu_ln_b, sgu_w, sgu_b, sc_conv_w, cf_conv_w, cf_conv_b,
           cf_ln_g, cf_ln_b, pool_w, pool_scale, w_out, norm_ffn_g, w_gate, w_up, w_down, norm_ple_g,
           w_ple_gate, w_ple_proj, final_norm_g):
    batch, seq_len, d_model = x.shape
    depth = w_in.shape[0]
    m = batch * seq_len
    bf16 = jnp.bfloat16
    row = lambda a: a.reshape(depth, 1, a.shape[-1])

    h = x.reshape(m, d_model)
    p2 = p.reshape(depth, m, PLE_DIM)
    w_in_b, w_out_b = w_in.astype(bf16), w_out.astype(bf16)
    w_gate_b, w_up_b, w_down_b = w_gate.astype(bf16), w_up.astype(bf16), w_down.astype(bf16)
    w_pg_b, w_pp_b, pool_w_b = w_ple_gate.astype(bf16), w_ple_proj.astype(bf16), pool_w.astype(bf16)
    sgu_bias = jnp.broadcast_to(sgu_b[..., None], sgu_b.shape + (HEAD_DIM,))
    mix_g, ffn_g, ple_g = row(norm_mix_g), row(norm_ffn_g), row(norm_ple_g)
    cf_b, cf_g, cf_beta, pool_s = row(cf_conv_b), row(cf_ln_g), row(cf_ln_b), row(pool_scale)
    final_g = final_norm_g.reshape(1, d_model)

    for layer in range(depth):
        c = _mix_call(h, layer, seq_len, mix_g, w_in_b, sgu_ln_g, sgu_ln_b, sgu_w, sgu_bias, sc_conv_w,
                      cf_conv_w, cf_b, cf_g, cf_beta, pool_w_b, pool_s)
        h = _proj_call(c, h, layer, w_out_b)
        h = _ffn_call(h, layer, ffn_g, w_gate_b, w_up_b, w_down_b)
        h = _ple_call(h, p2, layer, ple_g, w_pg_b, w_pp_b, final_g, final_norm=layer == depth - 1)
    return h.reshape(batch, seq_len, d_model)
```
